```python
import jax, jax.numpy as jnp
from jax import lax
import numpy as np

D_MODEL = 1024
BATCH = 8
SEQ = 4096
DEPTH = 2

GRID_W = 64
CTX_LEN = 256
MIX_WIDTH = D_MODEL
FNET_WIDTH = MIX_WIDTH // 2
FNET_GROUPS = 4
FNET_GROUP_DIM = FNET_WIDTH // FNET_GROUPS
GLA_WIDTH = MIX_WIDTH - FNET_WIDTH
GLA_HEADS = 4
GLA_DV = GLA_WIDTH // GLA_HEADS
GLA_DK = GLA_DV // 2
GLA_QK = GLA_HEADS * GLA_DK
GATE_RANK = 16
GATE_NORMALIZER = 16.0
GLA_CHUNK = 64
D_FF = 2816
N_EXPERTS = 8
TOP_K = 2
EXPERT_FF = 3584
N_DENSE = (DEPTH + 1) // 2
N_MOE = DEPTH // 2
EPS = 1e-6
SPLIT_POINTS = (FNET_WIDTH,
                FNET_WIDTH + GLA_QK,
                FNET_WIDTH + 2 * GLA_QK,
                FNET_WIDTH + 2 * GLA_QK + GLA_WIDTH,
                FNET_WIDTH + 2 * GLA_QK + 2 * GLA_WIDTH,
                FNET_WIDTH + 2 * GLA_QK + 2 * GLA_WIDTH + GATE_RANK)
IN_WIDTH = FNET_WIDTH + 2 * GLA_QK + 2 * GLA_WIDTH + 2 * GATE_RANK

kernel_name = 'hybrid_fnet_gla_moe_dit'


def rmsnorm(x, w):
    xf = x.astype(jnp.float32)
    y = xf * lax.rsqrt(jnp.mean(xf * xf, axis=-1, keepdims=True) + EPS)
    return y.astype(x.dtype) * w


def modulate(h, shift, scale):
    return h * (1 + scale) + shift


def fourier_mix(u):
    B, L, _ = u.shape
    ug = u.reshape(B, L, FNET_GROUPS, FNET_GROUP_DIM).astype(jnp.float32)
    y = jnp.fft.fftn(ug, axes=(1, 3), norm='ortho').real
    return y.reshape(B, L, FNET_WIDTH).astype(u.dtype)


def gla_chunked(q, k, v, log_a, s0):
    B, L, H, DK = q.shape
    DV = v.shape[-1]
    n = L // GLA_CHUNK

    def to_chunks(t):
        return jnp.moveaxis(t.astype(jnp.float32).reshape(B, n, GLA_CHUNK, H, t.shape[-1]), 1, 0)

    qs, ks, vs, gs = to_chunks(q), to_chunks(k), to_chunks(v), to_chunks(log_a)
    causal = jnp.tril(jnp.ones((GLA_CHUNK, GLA_CHUNK), dtype=bool))[None, :, :, None, None]

    def step(S, inp):
        qc, kc, vc, gc = inp
        b = jnp.cumsum(gc, axis=1)
        o_inter = jnp.einsum('bthk,bhkv->bthv', qc * jnp.exp(b), S)
        diff = b[:, :, None] - b[:, None, :]
        decay = jnp.where(causal, jnp.exp(jnp.where(causal, diff, 0.0)), 0.0)
        scores = jnp.einsum('bthk,bshk,btshk->bhts', qc, kc, decay)
        o_intra = jnp.einsum('bhts,bshv->bthv', scores, vc)
        b_last = b[:, -1]
        S_new = jnp.exp(b_last)[..., None] * S + jnp.einsum(
            'bshk,bshv->bhkv', kc * jnp.exp(b_last[:, None] - b), vc)
        return S_new, o_inter + o_intra

    s_fin, o = lax.scan(step, s0.astype(jnp.float32), (qs, ks, vs, gs))
    o = jnp.moveaxis(o, 0, 1).reshape(B, L, H, DV)
    return o, s_fin


def split_projection(p, w_gate_f, b_gate_f, w_gate_b, b_gate_b):
    B, L, _ = p.shape
    u, q, k, v, g, lf, lb = jnp.split(p, SPLIT_POINTS, axis=-1)
    q = q.reshape(B, L, GLA_HEADS, GLA_DK) * (GLA_DK ** -0.5)
    k = k.reshape(B, L, GLA_HEADS, GLA_DK)
    v = v.reshape(B, L, GLA_HEADS, GLA_DV)
    log_f = jax.nn.log_sigmoid((lf @ w_gate_f + b_gate_f).astype(jnp.float32)) / GATE_NORMALIZER
    log_b = jax.nn.log_sigmoid((lb @ w_gate_b + b_gate_b).astype(jnp.float32)) / GATE_NORMALIZER
    log_f = log_f.reshape(B, L, GLA_HEADS, GLA_DK)
    log_b = log_b.reshape(B, L, GLA_HEADS, GLA_DK)
    return u, q, k, v, g, log_f, log_b


def gla_output(o, g, gla_norm_w):
    B, L = o.shape[:2]
    on = o * lax.rsqrt(jnp.mean(o * o, axis=-1, keepdims=True) + EPS)
    on = on.astype(g.dtype) * gla_norm_w
    return on.reshape(B, L, GLA_WIDTH) * jax.nn.silu(g)


def token_mixer(hx, hc, w_in, w_gate_f, b_gate_f, w_gate_b, b_gate_b, gla_norm_w, w_out, ctx_out):
    ux, qx, kx, vx, gx, lfx, lbx = split_projection(hx @ w_in, w_gate_f, b_gate_f, w_gate_b, b_gate_b)
    uc, qc, kc, vc, gc, lfc, lbc = split_projection(hc @ w_in, w_gate_f, b_gate_f, w_gate_b, b_gate_b)
    flip = lambda t: jnp.flip(t, axis=1)
    s0 = jnp.zeros((hx.shape[0], GLA_HEADS, GLA_DK, GLA_DV), jnp.float32)
    oc_f, sc_f = gla_chunked(qc, kc, vc, lfc, s0)
    ox_f, _ = gla_chunked(qx, kx, vx, lfx, sc_f)
    oc_b, sc_b = gla_chunked(flip(qc), flip(kc), flip(vc), flip(lbc), s0)
    ox_b, _ = gla_chunked(flip(qx), flip(kx), flip(vx), flip(lbx), sc_b)
    yx = gla_output(ox_f + flip(ox_b), gx, gla_norm_w)
    out_x = jnp.concatenate([fourier_mix(ux), yx], axis=-1) @ w_out
    if ctx_out:
        yc = gla_output(oc_f + flip(oc_b), gc, gla_norm_w)
        out_c = jnp.concatenate([fourier_mix(uc), yc], axis=-1) @ w_out
    else:
        out_c = None
    return out_x, out_c


def swiglu(h, w1, w3, w2):
    return (jax.nn.silu(h @ w1) * (h @ w3)) @ w2


def moe_swiglu(h, router_w, w1, w3, w2):
    B, L, D = h.shape
    t = h.reshape(B * L, D)
    logits = (t @ router_w).astype(jnp.float32)
    top_v, top_i = lax.top_k(logits, TOP_K)
    gates = jax.nn.softmax(top_v, axis=-1)
    combine = jnp.einsum('tk,tke->te', gates, jax.nn.one_hot(top_i, N_EXPERTS, dtype=jnp.float32))
    combine = combine.astype(t.dtype)
    out = jnp.zeros_like(t)
    for e in range(N_EXPERTS):
        out = out + combine[:, e:e + 1] * swiglu(t, w1[e], w3[e], w2[e])
    return out.reshape(B, L, D)


def channel_mixer(h, l, ffn_w1, ffn_w3, ffn_w2, router_w, moe_w1, moe_w3, moe_w2):
    i = l // 2
    if l % 2 == 0:
        return swiglu(h, ffn_w1[i], ffn_w3[i], ffn_w2[i])
    return moe_swiglu(h, router_w[i], moe_w1[i], moe_w3[i], moe_w2[i])


def setup_inputs(seed: int = 0) -> dict:
    key = jax.random.key(seed)
    ks = jax.random.split(key, 26)
    nrm = lambda k, shape, s: jax.random.normal(k, shape, jnp.float32) * s
    D = D_MODEL
    return {
        'x': nrm(ks[0], (BATCH, SEQ, D), 1.0),
        'c': nrm(ks[1], (BATCH, D), 1.0),
        'ctx': nrm(ks[2], (BATCH, CTX_LEN, D), 1.0),
        'c_ctx': nrm(ks[3], (D,), 1.0),
        'w_mod': nrm(ks[4], (DEPTH, D, 6 * D), 0.5 * D ** -0.5),
        'b_mod': nrm(ks[5], (DEPTH, 6 * D), 0.02),
        'norm_mix_w': 1.0 + nrm(ks[6], (DEPTH, D), 0.02),
        'norm_ffn_w': 1.0 + nrm(ks[7], (DEPTH, D), 0.02),
        'w_in': nrm(ks[8], (DEPTH, D, IN_WIDTH), D ** -0.5),
        'w_gate_f': nrm(ks[9], (DEPTH, GATE_RANK, GLA_QK), GATE_RANK ** -0.5),
        'b_gate_f': nrm(ks[10], (DEPTH, GLA_QK), 0.5),
        'w_gate_b': nrm(ks[11], (DEPTH, GATE_RANK, GLA_QK), GATE_RANK ** -0.5),
        'b_gate_b': nrm(ks[12], (DEPTH, GLA_QK), 0.5),
        'gla_norm_w': 1.0 + nrm(ks[13], (DEPTH, GLA_DV), 0.02),
        'w_out': nrm(ks[14], (DEPTH, MIX_WIDTH, D), MIX_WIDTH ** -0.5),
        'ffn_w1': nrm(ks[15], (N_DENSE, D, D_FF), D ** -0.5),
        'ffn_w3': nrm(ks[16], (N_DENSE, D, D_FF), D ** -0.5),
        'ffn_w2': nrm(ks[17], (N_DENSE, D_FF, D), D_FF ** -0.5),
        'router_w': nrm(ks[18], (N_MOE, D, N_EXPERTS), D ** -0.5),
        'moe_w1': nrm(ks[19], (N_MOE, N_EXPERTS, D, EXPERT_FF), D ** -0.5),
        'moe_w3': nrm(ks[20], (N_MOE, N_EXPERTS, D, EXPERT_FF), D ** -0.5),
        'moe_w2': nrm(ks[21], (N_MOE, N_EXPERTS, EXPERT_FF, D), EXPERT_FF ** -0.5),
        'final_norm_w': 1.0 + nrm(ks[22], (D,), 0.02),
    }


def reference(x, c, ctx, c_ctx, w_mod, b_mod, norm_mix_w, norm_ffn_w, w_in,
              w_gate_f, b_gate_f, w_gate_b, b_gate_b, gla_norm_w, w_out,
              ffn_w1, ffn_w3, ffn_w2, router_w, moe_w1, moe_w3, moe_w2, final_norm_w):
    silu_c = jax.nn.silu(c)
    silu_cc = jax.nn.silu(c_ctx)
    for l in range(DEPTH):
        last = l == DEPTH - 1
        mod_x = (silu_c @ w_mod[l] + b_mod[l])[:, None, :]
        mod_c = (silu_cc @ w_mod[l] + b_mod[l])[None, None, :]
        sh1x, sc1x, g1x, sh2x, sc2x, g2x = jnp.split(mod_x, 6, axis=-1)
        sh1c, sc1c, g1c, sh2c, sc2c, g2c = jnp.split(mod_c, 6, axis=-1)
        hx = modulate(rmsnorm(x, norm_mix_w[l]), sh1x, sc1x)
        hc = modulate(rmsnorm(ctx, norm_mix_w[l]), sh1c, sc1c)
        mx, mc = token_mixer(hx, hc, w_in[l], w_gate_f[l], b_gate_f[l], w_gate_b[l], b_gate_b[l],
                             gla_norm_w[l], w_out[l], not last)
        x = x + g1x * mx
        hx = modulate(rmsnorm(x, norm_ffn_w[l]), sh2x, sc2x)
        x = x + g2x * channel_mixer(hx, l, ffn_w1, ffn_w3, ffn_w2, router_w, moe_w1, moe_w3, moe_w2)
        if not last:
            ctx = ctx + g1c * mc
            hc = modulate(rmsnorm(ctx, norm_ffn_w[l]), sh2c, sc2c)
            ctx = ctx + g2c * channel_mixer(hc, l, ffn_w1, ffn_w3, ffn_w2, router_w, moe_w1, moe_w3, moe_w2)
    return rmsnorm(x, final_norm_w)
```

```python
import functools
import math

import numpy as np
import jax
import jax.numpy as jnp
from jax import lax
from jax.experimental import pallas as pl
from jax.experimental.pallas import tpu as pltpu

D_MODEL = 1024
FNET_WIDTH = 512
FNET_GROUP_DIM = 128
GLA_HEADS = 4
GLA_DV = 128
GLA_DK = 64
GLA_QK = GLA_HEADS * GLA_DK
GLA_WIDTH = GLA_HEADS * GLA_DV
GATE_RANK = 16
GATE_NORMALIZER = 16.0
GLA_CHUNK = 64
GLA_BLOCK = 256
MAIN_WIDTH = FNET_WIDTH + 2 * GLA_QK + 2 * GLA_WIDTH
N_EXPERTS = 8
ROUTER_LANES = 128
EPS = 1e-6

VMEM_LIMIT_BYTES = 56 * 1024 * 1024

F32 = jnp.float32
BF16 = jnp.bfloat16


def _params(semantics, vmem=VMEM_LIMIT_BYTES):
    return pltpu.CompilerParams(dimension_semantics=semantics, vmem_limit_bytes=vmem)


def _silu(x):
    return x * (1.0 / (1.0 + jnp.exp(-x)))


def _log_sigmoid(x):
    return jnp.minimum(x, 0.0) - jnp.log1p(jnp.exp(-jnp.abs(x)))


def _rms(x):
    return x * lax.rsqrt(jnp.mean(x * x, axis=-1, keepdims=True) + EPS)


def _dot(a, b):
    return jnp.dot(a, b, preferred_element_type=F32)


def _mod_kernel(c_ref, w_ref, b_ref, o_ref):
    s = _silu(c_ref[...]).astype(BF16)
    o_ref[0] = _dot(s, w_ref[0].astype(BF16)) + b_ref[0]


def _modulation(cond, w_mod, b_mod):
    depth, d, n = w_mod.shape
    rows = cond.shape[0]
    tn = 1536
    return pl.pallas_call(
        _mod_kernel,
        grid=(depth, n // tn),
        in_specs=[
            pl.BlockSpec((rows, d), lambda l, j: (0, 0)),
            pl.BlockSpec((1, d, tn), lambda l, j: (l, 0, j)),
            pl.BlockSpec((1, 1, tn), lambda l, j: (l, 0, j)),
        ],
        out_specs=pl.BlockSpec((1, rows, tn), lambda l, j: (l, 0, j)),
        out_shape=jax.ShapeDtypeStruct((depth, rows, n), F32),
        compiler_params=_params(("arbitrary", "arbitrary")),
        name="adaln_mod",
    )(cond, w_mod, b_mod.reshape(depth, 1, n))


def _in_proj_kernel(x_ref, nw_ref, sh_ref, sc_ref, wm_ref, wl_ref, wg_ref, bg_ref,
                    u_ref, q_ref, k_ref, v_ref, g_ref, lf_ref, lb_ref):
    x = x_ref[0]
    h = (_rms(x) * nw_ref[...]) * (1.0 + sc_ref[0]) + sh_ref[0]
    hb = h.astype(BF16)
    p = _dot(hb, wm_ref[...])
    low = _dot(hb, wl_ref[...])
    gates = _dot(low.astype(BF16), wg_ref[...]) + bg_ref[...]
    logg = _log_sigmoid(gates) / GATE_NORMALIZER
    o = 0
    u_ref[0] = p[:, o:o + FNET_WIDTH].astype(BF16); o += FNET_WIDTH
    q_ref[0] = p[:, o:o + GLA_QK] * (GLA_DK ** -0.5); o += GLA_QK
    k_ref[0] = p[:, o:o + GLA_QK]; o += GLA_QK
    v_ref[0] = p[:, o:o + GLA_WIDTH].astype(BF16); o += GLA_WIDTH
    g_ref[0] = p[:, o:o + GLA_WIDTH]
    lf_ref[0] = logg[:, :GLA_QK]
    lb_ref[0] = logg[:, GLA_QK:]


def _in_proj(x, norm_w, shift, scale, w_main, w_low, w_gate, b_gate, tm):
    nb, length, d = x.shape
    tok = lambda w: pl.BlockSpec((1, tm, w), lambda b, t: (b, t, 0))
    per_batch = pl.BlockSpec((1, 1, d), lambda b, t: (b, 0, 0))
    whole = lambda a: pl.BlockSpec(a.shape, lambda b, t: (0,) * a.ndim)
    out = lambda w, dt: jax.ShapeDtypeStruct((nb, length, w), dt)
    return pl.pallas_call(
        _in_proj_kernel,
        grid=(nb, length // tm),
        in_specs=[tok(d), whole(norm_w), per_batch, per_batch,
                  whole(w_main), whole(w_low), whole(w_gate), whole(b_gate)],
        out_specs=[tok(FNET_WIDTH), tok(GLA_QK), tok(GLA_QK), tok(GLA_WIDTH),
                   tok(GLA_WIDTH), tok(GLA_QK), tok(GLA_QK)],
        out_shape=[out(FNET_WIDTH, BF16), out(GLA_QK, F32), out(GLA_QK, F32), out(GLA_WIDTH, BF16),
                   out(GLA_WIDTH, F32), out(GLA_QK, F32), out(GLA_QK, F32)],
        compiler_params=_params(("arbitrary", "arbitrary")),
        name="in_proj",
    )(x, norm_w, shift, scale, w_main, w_low, w_gate, b_gate)


def _exact_cumsum(tri, g):
    g1 = g.astype(BF16)
    r1 = g - g1.astype(F32)
    g2 = r1.astype(BF16)
    g3 = (r1 - g2.astype(F32)).astype(BF16)
    return _dot(tri, g1) + _dot(tri, g2) + _dot(tri, g3)


def _gla_block(q, k, v, g, tri, state_ref, reverse):
    nc = GLA_BLOCK // GLA_CHUNK
    c = GLA_CHUNK
    b = _exact_cumsum(tri, g).reshape(nc, c, GLA_QK)
    end = 0 if reverse else c - 1
    mid = c // 2
    b_end = b[:, end:end + 1, :]
    b_mid = b[:, mid:mid + 1, :]
    q3 = q.reshape(nc, c, GLA_QK)
    k3 = k.reshape(nc, c, GLA_QK)
    q_in = (q3 * jnp.exp(b)).astype(BF16)
    q_md = (q3 * jnp.exp(b - b_mid)).astype(BF16)
    k_md = (k3 * jnp.exp(b_mid - b)).astype(BF16)
    k_out = (k3 * jnp.exp(b_end - b)).astype(BF16)
    dec = jnp.exp(b_end)
    v3 = v.reshape(nc, c, GLA_WIDTH)
    row = lax.broadcasted_iota(jnp.int32, (c, c), 0)
    col = lax.broadcasted_iota(jnp.int32, (c, c), 1)
    visible = (col >= row) if reverse else (col <= row)
    nt = (((1,), (1,)), ((), ()))
    tn = (((0,), (0,)), ((), ()))
    outs = [None] * nc
    order = range(nc - 1, -1, -1) if reverse else range(nc)
    for j in order:
        st = state_ref[...]
        stb = st.astype(BF16)
        heads = []
        new_cols = []
        for h in range(GLA_HEADS):
            ks = slice(h * GLA_DK, (h + 1) * GLA_DK)
            vs = slice(h * GLA_DV, (h + 1) * GLA_DV)
            scores = lax.dot_general(q_md[j][:, ks], k_md[j][:, ks], nt, preferred_element_type=F32)
            scores = jnp.where(visible, scores, 0.0).astype(BF16)
            vh = v3[j][:, vs]
            o = _dot(scores, vh)
            o = o + lax.dot_general(q_in[j][:, ks], stb[:, ks], nt, preferred_element_type=F32)
            heads.append(o)
            new_cols.append(lax.dot_general(vh, k_out[j][:, ks], tn, preferred_element_type=F32))
        outs[j] = jnp.concatenate(heads, axis=-1)
        state_ref[...] = st * dec[j] + jnp.concatenate(new_cols, axis=-1)
    return jnp.concatenate(outs, axis=0)


def _gla_kernel(qc_ref, kc_ref, vc_ref, fc_ref, bc_ref,
                qf_ref, kf_ref, vf_ref, ff_ref,
                qb_ref, kb_ref, vb_ref, bb_ref,
                tl_ref, tu_ref,
                ofc_ref, obc_ref, ofx_ref, obx_ref,
                sf_ref, sb_ref):
    s = pl.program_id(1)

    @pl.when(s == 0)
    def _():
        sf_ref[...] = jnp.zeros_like(sf_ref)
        sb_ref[...] = jnp.zeros_like(sb_ref)
        ofc_ref[0] = _gla_block(qc_ref[0], kc_ref[0], vc_ref[0], fc_ref[0], tl_ref[...], sf_ref, False)
        obc_ref[0] = _gla_block(qc_ref[0], kc_ref[0], vc_ref[0], bc_ref[0], tu_ref[...], sb_ref, True)

    @pl.when(s > 0)
    def _():
        ofx_ref[0] = _gla_block(qf_ref[0], kf_ref[0], vf_ref[0], ff_ref[0], tl_ref[...], sf_ref, False)
        obx_ref[0] = _gla_block(qb_ref[0], kb_ref[0], vb_ref[0], bb_ref[0], tu_ref[...], sb_ref, True)


def _chunk_triangles():
    t = np.arange(GLA_BLOCK)
    same = (t[:, None] // GLA_CHUNK) == (t[None, :] // GLA_CHUNK)
    lower = same & (t[None, :] <= t[:, None])
    upper = same & (t[None, :] >= t[:, None])
    return jnp.asarray(lower, BF16), jnp.asarray(upper, BF16)


def _gla(ctx_parts, x_parts):
    qc, kc, vc, lfc, lbc = ctx_parts
    qx, kx, vx, lfx, lbx = x_parts
    nb, lx, _ = qx.shape
    nx = lx // GLA_BLOCK
    tl, tu = _chunk_triangles()
    cblk = lambda w: pl.BlockSpec((1, GLA_BLOCK, w), lambda b, s: (b, 0, 0))
    fblk = lambda w: pl.BlockSpec((1, GLA_BLOCK, w), lambda b, s: (b, jnp.maximum(s - 1, 0), 0))
    bblk = lambda w: pl.BlockSpec((1, GLA_BLOCK, w), lambda b, s: (b, jnp.minimum(nx - s, nx - 1), 0))
    tri = pl.BlockSpec((GLA_BLOCK, GLA_BLOCK), lambda b, s: (0, 0))
    o_c = jax.ShapeDtypeStruct((nb, GLA_BLOCK, GLA_WIDTH), F32)
    o_x = jax.ShapeDtypeStruct((nb, lx, GLA_WIDTH), F32)
    return pl.pallas_call(
        _gla_kernel,
        grid=(nb, nx + 1),
        in_specs=[cblk(GLA_QK), cblk(GLA_QK), cblk(GLA_WIDTH), cblk(GLA_QK), cblk(GLA_QK),
                  fblk(GLA_QK), fblk(GLA_QK), fblk(GLA_WIDTH), fblk(GLA_QK),
                  bblk(GLA_QK), bblk(GLA_QK), bblk(GLA_WIDTH), bblk(GLA_QK),
                  tri, tri],
        out_specs=[cblk(GLA_WIDTH), cblk(GLA_WIDTH), fblk(GLA_WIDTH), bblk(GLA_WIDTH)],
        out_shape=[o_c, o_c, o_x, o_x],
        scratch_shapes=[pltpu.VMEM((GLA_DV, GLA_QK), F32), pltpu.VMEM((GLA_DV, GLA_QK), F32)],
        compiler_params=_params(("arbitrary", "arbitrary")),
        name="gla_scan",
    )(qc, kc, vc, lfc, lbc, qx, kx, vx, lfx, qx, kx, vx, lbx, tl, tu)


def _fnet_tables(length):
    n = int(round(math.sqrt(length)))
    assert n * n == length
    j = np.arange(FNET_GROUP_DIM)
    ang = 2.0 * np.pi * np.outer(j, j) / FNET_GROUP_DIM
    chan = np.concatenate([np.cos(ang), -np.sin(ang)], axis=1) / math.sqrt(FNET_GROUP_DIM)
    r = np.arange(n)
    a = 2.0 * np.pi * np.outer(r, r) / n
    cn, sn = np.cos(a) / math.sqrt(n), np.sin(a) / math.sqrt(n)
    stage1 = np.block([[cn, sn], [-sn, cn]])
    stage2 = np.concatenate([cn, sn], axis=1)
    tw = 2.0 * np.pi * np.outer(r, r).reshape(-1) / length
    twc = np.repeat(np.cos(tw)[:, None], FNET_GROUP_DIM, axis=1)
    tws = np.repeat(np.sin(tw)[:, None], FNET_GROUP_DIM, axis=1)
    as32 = lambda m: jnp.asarray(m.astype(np.float32))
    return n, as32(chan).astype(BF16), as32(stage1).astype(BF16), as32(stage2).astype(BF16), as32(twc), as32(tws)


def _fnet_kernel(n, u_ref, chan_ref, s1_ref, s2_ref, twc_ref, tws_ref, y_ref, vr_ref, vi_ref, ar_ref, ai_ref):
    length = n * n
    groups = vr_ref.shape[0]
    rows = min(length, 512)
    for r0 in range(0, length, rows):
        for gi in range(groups):
            gs = slice(gi * FNET_GROUP_DIM, (gi + 1) * FNET_GROUP_DIM)
            vv = _dot(u_ref[0, r0:r0 + rows, gs], chan_ref[...])
            vr_ref[gi, r0:r0 + rows, :] = vv[:, :FNET_GROUP_DIM]
            vi_ref[gi, r0:r0 + rows, :] = vv[:, FNET_GROUP_DIM:]
    strided = lambda ref, start: jnp.concatenate(
        [ref[gi, pl.ds(start, n, stride=n), :] for gi in range(groups)], axis=1)
    for c in range(n):
        stack = jnp.concatenate([strided(vr_ref, c), strided(vi_ref, c)], axis=0)
        a = _dot(s1_ref[...], stack.astype(BF16))
        tc = twc_ref[c * n:(c + 1) * n, :]
        ts = tws_ref[c * n:(c + 1) * n, :]
        for gi in range(groups):
            gs = slice(gi * FNET_GROUP_DIM, (gi + 1) * FNET_GROUP_DIM)
            a_r, a_i = a[:n, gs], a[n:, gs]
            ar_ref[gi, c * n:(c + 1) * n, :] = a_r * tc + a_i * ts
            ai_ref[gi, c * n:(c + 1) * n, :] = a_i * tc - a_r * ts
    for f1 in range(n):
        stack = jnp.concatenate([strided(ar_ref, f1), strided(ai_ref, f1)], axis=0)
        y = _dot(s2_ref[...], stack.astype(BF16))
        for gi in range(groups):
            y_ref[0, gi, pl.ds(f1, n, stride=n), :] = y[:, gi * FNET_GROUP_DIM:(gi + 1) * FNET_GROUP_DIM]


def _fnet(u):
    nb, length, width = u.shape
    n, chan, s1, s2, twc, tws = _fnet_tables(length)
    groups = width // FNET_GROUP_DIM
    gstep = 2
    whole = lambda a: pl.BlockSpec(a.shape, lambda b, j: (0,) * a.ndim)
    return pl.pallas_call(
        functools.partial(_fnet_kernel, n),
        grid=(nb, groups // gstep),
        in_specs=[pl.BlockSpec((1, length, gstep * FNET_GROUP_DIM), lambda b, j: (b, 0, j)),
                  whole(chan), whole(s1), whole(s2), whole(twc), whole(tws)],
        out_specs=pl.BlockSpec((1, gstep, length, FNET_GROUP_DIM), lambda b, j: (b, j, 0, 0)),
        out_shape=jax.ShapeDtypeStruct((nb, groups, length, FNET_GROUP_DIM), F32),
        scratch_shapes=[pltpu.VMEM((gstep, length, FNET_GROUP_DIM), F32)] * 4,
        compiler_params=_params(("arbitrary", "arbitrary")),
        name=f"fnet_{length}",
    )(u, chan, s1, s2, twc, tws)


def _out_proj_kernel(y_ref, of_ref, ob_ref, g_ref, x_ref, gnw_ref, wo_ref, g1_ref,
                     nw_ref, sh_ref, sc_ref, x1_ref, h_ref):
    o = of_ref[0] + ob_ref[0]
    gate = _silu(g_ref[0])
    parts = [y_ref[0, gi] for gi in range(y_ref.shape[1])]
    for h in range(GLA_HEADS):
        hs = slice(h * GLA_DV, (h + 1) * GLA_DV)
        parts.append((_rms(o[:, hs]) * gnw_ref[...]) * gate[:, hs])
    mixed = _dot(jnp.concatenate(parts, axis=-1).astype(BF16), wo_ref[...])
    x1 = x_ref[0] + g1_ref[0] * mixed
    x1_ref[0] = x1
    h_ref[0] = ((_rms(x1) * nw_ref[...]) * (1.0 + sc_ref[0]) + sh_ref[0]).astype(BF16)


def _out_proj(y, o_f, o_b, g, x, gla_norm_w, w_out, gate1, norm_w, shift, scale, tm):
    nb, length, d = x.shape
    tok = lambda w: pl.BlockSpec((1, tm, w), lambda b, t: (b, t, 0))
    per_batch = pl.BlockSpec((1, 1, d), lambda b, t: (b, 0, 0))
    whole = lambda a: pl.BlockSpec(a.shape, lambda b, t: (0,) * a.ndim)
    fourier = pl.BlockSpec((1, y.shape[1], tm, FNET_GROUP_DIM), lambda b, t: (b, 0, t, 0))
    return pl.pallas_call(
        _out_proj_kernel,
        grid=(nb, length // tm),
        in_specs=[fourier, tok(GLA_WIDTH), tok(GLA_WIDTH), tok(GLA_WIDTH), tok(d),
                  whole(gla_norm_w), whole(w_out), per_batch, whole(norm_w), per_batch, per_batch],
        out_specs=[tok(d), tok(d)],
        out_shape=[jax.ShapeDtypeStruct((nb, length, d), F32), jax.ShapeDtypeStruct((nb, length, d), BF16)],
        compiler_params=_params(("arbitrary", "arbitrary")),
        name="out_proj",
    )(y, o_f, o_b, g, x, gla_norm_w, w_out, gate1, norm_w, shift, scale)


def _ffn_kernel(fk, h_ref, x_ref, g2_ref, w1_ref, w3_ref, w2_ref, o_ref, acc_ref):
    h = h_ref[0]
    d_ff = w1_ref.shape[1]
    for i, f0 in enumerate(range(0, d_ff, fk)):
        a = _dot(h, w1_ref[:, f0:f0 + fk])
        b = _dot(h, w3_ref[:, f0:f0 + fk])
        part = _dot((_silu(a) * b).astype(BF16), w2_ref[f0:f0 + fk, :])
        if i == 0:
            acc_ref[...] = part
        else:
            acc_ref[...] += part
    o_ref[0] = x_ref[0] + g2_ref[0] * acc_ref[...]


def _ffn(h, x, gate2, w1, w3, w2, tm, fk=256):
    nb, length, d = x.shape
    tok = pl.BlockSpec((1, tm, d), lambda b, t: (b, t, 0))
    per_batch = pl.BlockSpec((1, 1, d), lambda b, t: (b, 0, 0))
    whole = lambda a: pl.BlockSpec(a.shape, lambda b, t: (0,) * a.ndim, pipeline_mode=pl.Buffered(1))
    return pl.pallas_call(
        functools.partial(_ffn_kernel, fk),
        grid=(nb, length // tm),
        in_specs=[tok, tok, per_batch, whole(w1), whole(w3), whole(w2)],
        out_specs=tok,
        out_shape=jax.ShapeDtypeStruct((nb, length, d), F32),
        scratch_shapes=[pltpu.VMEM((tm, d), F32)],
        compiler_params=_params(("arbitrary", "arbitrary")),
        name="dense_ffn",
    )(h, x, gate2, w1, w3, w2)


def _moe_kernel(h_ref, x_ref, g2_ref, rw_ref, w1_ref, w3_ref, w2_ref, fw_ref, o_ref,
                comb_ref, eacc_ref, acc_ref):
    e = pl.program_id(2)
    f = pl.program_id(3)
    last_e = pl.num_programs(2) - 1
    last_f = pl.num_programs(3) - 1
    h = h_ref[0]

    @pl.when((e == 0) & (f == 0))
    def _():
        lane = lax.broadcasted_iota(jnp.int32, comb_ref.shape, 1).astype(F32)
        logits = jnp.where(lane < N_EXPERTS, _dot(h, rw_ref[...]), -jnp.inf)
        m1 = jnp.max(logits, axis=-1, keepdims=True)
        i1 = jnp.min(jnp.where(logits == m1, lane, float(ROUTER_LANES)), axis=-1, keepdims=True)
        first = lane == i1
        rest = jnp.where(first, -jnp.inf, logits)
        m2 = jnp.max(rest, axis=-1, keepdims=True)
        i2 = jnp.min(jnp.where(rest == m2, lane, float(ROUTER_LANES)), axis=-1, keepdims=True)
        second = lane == i2
        z = jnp.exp(m2 - m1)
        comb_ref[...] = jnp.where(first, 1.0 / (1.0 + z), 0.0) + jnp.where(second, z / (1.0 + z), 0.0)
        acc_ref[...] = jnp.zeros_like(acc_ref)

    a = _dot(h, w1_ref[0])
    b = _dot(h, w3_ref[0])
    part = _dot((_silu(a) * b).astype(BF16), w2_ref[0])

    @pl.when(f == 0)
    def _():
        eacc_ref[...] = part

    @pl.when(f > 0)
    def _():
        eacc_ref[...] += part

    @pl.when(f == last_f)
    def _():
        lane = lax.broadcasted_iota(jnp.int32, comb_ref.shape, 1)
        weight = jnp.sum(jnp.where(lane == e, comb_ref[...], 0.0), axis=-1, keepdims=True)
        acc_ref[...] += weight * eacc_ref[...]

    @pl.when((e == last_e) & (f == last_f))
    def _():
        x2 = x_ref[0] + g2_ref[0] * acc_ref[...]
        o_ref[0] = _rms(x2) * fw_ref[...]


def _moe(h, x, gate2, router_w, w1, w3, w2, final_w, tm=1024, fk=512):
    nb, length, d = x.shape
    ne, _, d_ff = w1.shape
    tok = pl.BlockSpec((1, tm, d), lambda b, t, e, f: (b, t, 0))
    per_batch = pl.BlockSpec((1, 1, d), lambda b, t, e, f: (b, 0, 0))
    whole = lambda a: pl.BlockSpec(a.shape, lambda b, t, e, f: (0,) * a.ndim)
    return pl.pallas_call(
        _moe_kernel,
        grid=(nb, length // tm, ne, d_ff // fk),
        in_specs=[tok, tok, per_batch, whole(router_w),
                  pl.BlockSpec((1, d, fk), lambda b, t, e, f: (e, 0, f)),
                  pl.BlockSpec((1, d, fk), lambda b, t, e, f: (e, 0, f)),
                  pl.BlockSpec((1, fk, d), lambda b, t, e, f: (e, f, 0)),
                  whole(final_w)],
        out_specs=tok,
        out_shape=jax.ShapeDtypeStruct((nb, length, d), F32),
        scratch_shapes=[pltpu.VMEM((tm, ROUTER_LANES), F32), pltpu.VMEM((tm, d), F32), pltpu.VMEM((tm, d), F32)],
        compiler_params=_params(("arbitrary",) * 4),
        name="moe_ffn",
    )(h, x, gate2, router_w, w1, w3, w2, final_w)


def kernel(x, c, ctx, c_ctx, w_mod, b_mod, norm_mix_w, norm_ffn_w, w_in, w_gate_f, b_gate_f, w_gate_b,
           b_gate_b, gla_norm_w, w_out, ffn_w1, ffn_w3, ffn_w2, router_w, moe_w1, moe_w3, moe_w2, final_norm_w):
    depth = w_mod.shape[0]
    nb, seq, d = x.shape
    ctx_len = ctx.shape[1]
    assert ctx_len == GLA_BLOCK and seq % GLA_BLOCK == 0 and d == D_MODEL
    assert depth == 2, "dense layer then expert layer with the final norm fused into the expert kernel"

    cond = jnp.zeros((16, d), F32).at[:nb].set(c).at[nb].set(c_ctx)
    mod = _modulation(cond, w_mod, b_mod)

    row = lambda v: v.reshape(1, -1)
    zeros = jnp.zeros((GATE_RANK, GLA_QK), F32)

    for l in range(depth):
        last = l == depth - 1
        mx = [mod[l, :nb, i * d:(i + 1) * d].reshape(nb, 1, d) for i in range(6)]
        mc = [jnp.broadcast_to(mod[l, nb:nb + 1, i * d:(i + 1) * d].reshape(1, 1, d), (nb, 1, d)) for i in range(6)]
        w_main = w_in[l, :, :MAIN_WIDTH].astype(BF16)
        w_low = w_in[l, :, MAIN_WIDTH:].astype(BF16)
        w_gate = jnp.concatenate([jnp.concatenate([w_gate_f[l], zeros], axis=1),
                                  jnp.concatenate([zeros, w_gate_b[l]], axis=1)], axis=0).astype(BF16)
        b_gate = jnp.concatenate([b_gate_f[l], b_gate_b[l]]).reshape(1, -1)
        nmw = row(norm_mix_w[l])
        nfw = row(norm_ffn_w[l])
        gnw = row(gla_norm_w[l])
        wo = w_out[l].astype(BF16)

        ux, qx, kx, vx, gx, lfx, lbx = _in_proj(x, nmw, mx[0], mx[1], w_main, w_low, w_gate, b_gate, tm=512)
        uc, qc, kc, vc, gc, lfc, lbc = _in_proj(ctx, nmw, mc[0], mc[1], w_main, w_low, w_gate, b_gate, tm=ctx_len)
        ofc, obc, ofx, obx = _gla((qc, kc, vc, lfc, lbc), (qx, kx, vx, lfx, lbx))
        yx = _fnet(ux)
        x1, hx = _out_proj(yx, ofx, obx, gx, x, gnw, wo, mx[2], nfw, mx[3], mx[4], tm=512)

        if not last:
            yc = _fnet(uc)
            c1, hc = _out_proj(yc, ofc, obc, gc, ctx, gnw, wo, mc[2], nfw, mc[3], mc[4], tm=ctx_len)

        i = l // 2
        if l % 2 == 0:
            w1, w3, w2 = ffn_w1[i].astype(BF16), ffn_w3[i].astype(BF16), ffn_w2[i].astype(BF16)
            x = _ffn(hx, x1, mx[5], w1, w3, w2, tm=512)
            if not last:
                ctx = _ffn(hc, c1, mc[5], w1, w3, w2, tm=ctx_len)
        else:
            assert last
            rw = jnp.zeros((d, ROUTER_LANES), BF16).at[:, :N_EXPERTS].set(router_w[i].astype(BF16))
            x = _moe(hx, x1, mx[5], rw, moe_w1[i].astype(BF16), moe_w3[i].astype(BF16),
                     moe_w2[i].astype(BF16), row(final_norm_w))
    return x
```
